```python
import jax
import jax.numpy as jnp
from jax import lax
import numpy as np

D_MODEL = 2048
BATCH = 4
SEQ = 4096
DEPTH = 2

CHUNK = 64
N_EVEN = (DEPTH + 1) // 2
N_ODD = DEPTH // 2
NORM_EPS = 1e-6
CONV_WIDTH = 4

RG_WIDTH = D_MODEL // 2
RG_BLOCKS = 8
RG_BLOCK_DIM = RG_WIDTH // RG_BLOCKS
RG_C = 8.0

HG_HEADS = 8
HG_DK = 128
HG_DV = (D_MODEL // 2) // HG_HEADS
HG_QK_WIDTH = HG_HEADS * HG_DK
HG_V_WIDTH = HG_HEADS * HG_DV

AB_IN_SIZES = (RG_WIDTH, RG_WIDTH, HG_QK_WIDTH, HG_QK_WIDTH, HG_V_WIDTH, HG_V_WIDTH)
AB_IN_WIDTH = sum(AB_IN_SIZES)
AB_OUT_WIDTH = RG_WIDTH + HG_V_WIDTH

GDN_DK = 128
GDN_DV = 128
GDN_K_HEADS = D_MODEL // GDN_DK
GDN_V_HEADS = 2 * GDN_K_HEADS
GDN_QK_WIDTH = GDN_K_HEADS * GDN_DK
GDN_V_WIDTH = GDN_V_HEADS * GDN_DV
GDN_QKV_WIDTH = 2 * GDN_QK_WIDTH + GDN_V_WIDTH
C_IN_SIZES = (GDN_QKV_WIDTH, GDN_V_WIDTH, GDN_V_HEADS, GDN_V_HEADS)
C_IN_WIDTH = sum(C_IN_SIZES)

N_GROUPS = 8
EXPERTS_PER_GROUP = 8
N_EXPERTS = N_GROUPS * EXPERTS_PER_GROUP
TOP_K = 2
D_EXPERT = D_MODEL // 4
MOE_BLOCK = 128
ROUTER_BIAS_SCALE = 0.01

kernel_name = 'hybrid_rglru_hgrn2_gdn_hmoe'

F32 = jnp.float32


def _split(t, sizes):
    return jnp.split(t, np.cumsum(sizes)[:-1].tolist(), axis=-1)


def rms_norm(x, gain):
    x32 = x.astype(F32)
    y = x32 * lax.rsqrt(jnp.mean(x32 * x32, axis=-1, keepdims=True) + NORM_EPS)
    return (y * gain.astype(F32)).astype(x.dtype)


def l2_norm(x):
    return x * lax.rsqrt(jnp.sum(x * x, axis=-1, keepdims=True) + 1e-6)


def causal_depthwise_conv(x, w):
    S = x.shape[1]
    xp = jnp.pad(x, ((0, 0), (CONV_WIDTH - 1, 0), (0, 0)))
    w = w.astype(x.dtype)
    y = xp[:, 0:S, :] * w[0]
    for j in range(1, CONV_WIDTH):
        y = y + xp[:, j:j + S, :] * w[j]
    return y


def to_chunks(t):
    B, S, H, d = t.shape
    return t.reshape(B, S // CHUNK, CHUNK, H, d).transpose(1, 0, 3, 2, 4)


def from_chunks(t):
    nC, B, H, C, d = t.shape
    return t.transpose(1, 0, 3, 2, 4).reshape(B, nC * C, H, d)


def rg_lru(x, w_a, b_a, w_x, b_x, lam):
    B, S, W = x.shape
    x32 = x.astype(F32)
    xb = x32.reshape(B, S, RG_BLOCKS, RG_BLOCK_DIM)
    r = jax.nn.sigmoid(jnp.einsum('bshi,hij->bshj', xb, w_a.astype(F32)).reshape(B, S, W) + b_a.astype(F32))
    i = jax.nn.sigmoid(jnp.einsum('bshi,hij->bshj', xb, w_x.astype(F32)).reshape(B, S, W) + b_x.astype(F32))
    log_a = -RG_C * jax.nn.softplus(-lam.astype(F32)) * r
    a = jnp.exp(log_a)
    u = jnp.sqrt(-jnp.expm1(2.0 * log_a)) * (i * x32)

    def combine(left, right):
        a_l, u_l = left
        a_r, u_r = right
        return a_l * a_r, a_r * u_l + u_r

    _, h = lax.associative_scan(combine, (a, u), axis=1)
    return h


def hgrn2_chunkwise(q, f_logit, v, lb):
    B, S, H, DK = q.shape
    DV = v.shape[-1]
    q = jax.nn.silu(q.astype(F32))
    lb = lb.reshape(H, DK)
    f = lb + (1.0 - lb) * jax.nn.sigmoid(f_logit.astype(F32))
    k = 1.0 - f
    qc, kc, vc = to_chunks(q), to_chunks(k), to_chunks(v.astype(F32))
    bc = jnp.cumsum(to_chunks(jnp.log(f)), axis=3)
    pos = jnp.arange(CHUNK)
    causal = (pos[:, None] >= pos[None, :])[:, :, None]

    def step(state, inp):
        q_n, k_n, v_n, b_n = inp
        inter = jnp.einsum('bhtk,bhkv->bhtv', q_n * jnp.exp(b_n), state)
        diff = b_n[:, :, :, None, :] - b_n[:, :, None, :, :]
        decay = jnp.exp(jnp.where(causal, diff, -jnp.inf))
        scores = jnp.einsum('bhtk,bhsk,bhtsk->bhts', q_n, k_n, decay)
        intra = jnp.einsum('bhts,bhsv->bhtv', scores, v_n)
        b_last = b_n[:, :, -1:, :]
        state = (jnp.exp(b_last[:, :, 0, :])[..., None] * state
                 + jnp.einsum('bhsk,bhsv->bhkv', k_n * jnp.exp(b_last - b_n), v_n))
        return state, inter + intra

    _, o = lax.scan(step, jnp.zeros((B, H, DK, DV), F32), (qc, kc, vc, bc))
    return from_chunks(o)


def mixer_rglru_hgrn2(h, w_in, conv_w, conv_b, w_a, b_a, w_x, b_x, lam, lb, hg_gain, w_out):
    B, S, _ = h.shape
    x_rg, gate_rg, q_hg, f_hg, i_hg, g_hg = _split(h @ w_in, AB_IN_SIZES)
    x_rg = causal_depthwise_conv(x_rg, conv_w) + conv_b.astype(h.dtype)
    y_a = rg_lru(x_rg, w_a, b_a, w_x, b_x, lam) * jax.nn.gelu(gate_rg.astype(F32))
    o = hgrn2_chunkwise(q_hg.reshape(B, S, HG_HEADS, HG_DK),
                        f_hg.reshape(B, S, HG_HEADS, HG_DK),
                        i_hg.reshape(B, S, HG_HEADS, HG_DV), lb)
    o = rms_norm(o, hg_gain.reshape(HG_HEADS, HG_DV))
    y_b = o.reshape(B, S, HG_V_WIDTH) * jax.nn.silu(g_hg.astype(F32))
    y = jnp.concatenate([y_a, y_b], axis=-1).astype(h.dtype)
    return y @ w_out


def gated_delta_rule_chunkwise(q, k, v, g, beta):
    B, S, H, DK = q.shape
    DV = v.shape[-1]
    qc, kc, vc = to_chunks(q), to_chunks(k), to_chunks(v)
    gc = jnp.cumsum(to_chunks(g[..., None])[..., 0], axis=-1)
    bc = to_chunks(beta[..., None])[..., 0]
    pos = jnp.arange(CHUNK)
    incl = pos[:, None] >= pos[None, :]
    strict = pos[:, None] > pos[None, :]
    decay = jnp.exp(jnp.where(incl, gc[..., :, None] - gc[..., None, :], -jnp.inf))
    kk = jnp.einsum('nbhtd,nbhsd->nbhts', kc, kc)
    lower = jnp.where(strict, bc[..., :, None] * kk * decay, 0.0)
    eye = jnp.eye(CHUNK, dtype=F32)
    t_inv = lax.linalg.triangular_solve(lower + eye, jnp.broadcast_to(eye, lower.shape),
                                        left_side=True, lower=True, unit_diagonal=True)
    u = t_inv @ (vc * bc[..., None])
    w = t_inv @ (kc * (bc * jnp.exp(gc))[..., None])
    qk = jnp.einsum('nbhtd,nbhsd->nbhts', qc, kc) * decay
    q_dec = qc * jnp.exp(gc)[..., None]
    k_dec = kc * jnp.exp(gc[..., -1:] - gc)[..., None]
    g_last = jnp.exp(gc[..., -1])

    def step(state, inp):
        q_n, k_n, u_n, w_n, qk_n, gl_n = inp
        delta = u_n - w_n @ state
        out = q_n @ state + qk_n @ delta
        state = gl_n[..., None, None] * state + jnp.einsum('bhsk,bhsv->bhkv', k_n, delta)
        return state, out

    _, o = lax.scan(step, jnp.zeros((B, H, DK, DV), F32), (q_dec, k_dec, u, w, qk, g_last))
    return from_chunks(o)


def mixer_gated_deltanet(h, w_in, conv_w, a_log, dt_bias, out_gain, w_out):
    B, S, _ = h.shape
    qkv, z, b, a = _split(h @ w_in, C_IN_SIZES)
    qkv = jax.nn.silu(causal_depthwise_conv(qkv, conv_w)).astype(F32)
    q, k, v = _split(qkv, (GDN_QK_WIDTH, GDN_QK_WIDTH, GDN_V_WIDTH))
    rep = GDN_V_HEADS // GDN_K_HEADS
    q = jnp.repeat(l2_norm(q.reshape(B, S, GDN_K_HEADS, GDN_DK)), rep, axis=2) * (GDN_DK ** -0.5)
    k = jnp.repeat(l2_norm(k.reshape(B, S, GDN_K_HEADS, GDN_DK)), rep, axis=2)
    v = v.reshape(B, S, GDN_V_HEADS, GDN_DV)
    beta = jax.nn.sigmoid(b.astype(F32))
    g = -jnp.exp(a_log.astype(F32)) * jax.nn.softplus(a.astype(F32) + dt_bias.astype(F32))
    o = gated_delta_rule_chunkwise(q, k, v, g, beta)
    o = rms_norm(o, out_gain) * jax.nn.silu(z.astype(F32).reshape(B, S, GDN_V_HEADS, GDN_DV))
    return o.reshape(B, S, GDN_V_WIDTH).astype(h.dtype) @ w_out


def grouped_expert_ffn(xt, expert_id, w_gate, w_up, w_down):
    T, D = xt.shape
    K = expert_id.shape[1]
    M = T * K
    flat_e = expert_id.reshape(M)
    order = jnp.argsort(flat_e)
    sorted_e = flat_e[order]
    counts = jnp.bincount(flat_e, length=N_EXPERTS)
    padded = (counts + MOE_BLOCK - 1) // MOE_BLOCK * MOE_BLOCK
    pad_end = jnp.cumsum(padded)
    pad_start = pad_end - padded
    start = jnp.cumsum(counts) - counts
    dest = pad_start[sorted_e] + jnp.arange(M) - start[sorted_e]
    n_blocks = M // MOE_BLOCK + N_EXPERTS
    rows = jnp.zeros((n_blocks * MOE_BLOCK, D), xt.dtype).at[dest].set(xt[order // K])
    block_expert = jnp.minimum(
        jnp.searchsorted(pad_end, jnp.arange(n_blocks) * MOE_BLOCK, side='right'), N_EXPERTS - 1)

    def block_ffn(args):
        xb, e = args
        return (jax.nn.silu(xb @ w_gate[e]) * (xb @ w_up[e])) @ w_down[e]

    yb = lax.map(block_ffn, (rows.reshape(n_blocks, MOE_BLOCK, D), block_expert))
    y_sorted = yb.reshape(n_blocks * MOE_BLOCK, D)[dest]
    y = jnp.zeros((M, D), yb.dtype).at[order].set(y_sorted)
    return y.reshape(T, K, D)


def hierarchical_moe(h, w_grp, b_grp, w_exp, b_exp, w_gate, w_up, w_down):
    B, S, D = h.shape
    xt = h.reshape(B * S, D)
    T = xt.shape[0]
    grp_prob = jax.nn.softmax((xt @ w_grp).astype(F32) + b_grp.astype(F32), axis=-1)
    grp_p, grp_idx = lax.top_k(grp_prob, 1)
    exp_logits = ((xt @ w_exp).astype(F32) + b_exp.astype(F32)).reshape(T, N_GROUPS, EXPERTS_PER_GROUP)
    sel = jnp.broadcast_to(grp_idx[:, :, None], (T, 1, EXPERTS_PER_GROUP))
    in_grp = jnp.take_along_axis(exp_logits, sel, axis=1)[:, 0]
    top_logit, top_idx = lax.top_k(in_grp, TOP_K)
    gate = jax.nn.softmax(top_logit, axis=-1) * grp_p
    expert_id = grp_idx * EXPERTS_PER_GROUP + top_idx
    y = grouped_expert_ffn(xt, expert_id, w_gate, w_up, w_down)
    out = jnp.einsum('tk,tkd->td', gate.astype(y.dtype), y)
    return out.reshape(B, S, D)


def setup_inputs(seed: int = 0) -> dict:
    key = jax.random.key(seed)
    ks = jax.random.split(key, 32)

    def nrm(k, shape, scale):
        return jax.random.normal(k, shape, F32) * scale

    x = nrm(ks[0], (BATCH, SEQ, D_MODEL), 1.0)
    norm_mix = 1.0 + nrm(ks[1], (DEPTH, D_MODEL), 0.01)
    norm_ffn = 1.0 + nrm(ks[2], (DEPTH, D_MODEL), 0.01)
    norm_final = 1.0 + nrm(ks[3], (D_MODEL,), 0.01)
    ab_w_in = nrm(ks[4], (N_EVEN, D_MODEL, AB_IN_WIDTH), D_MODEL ** -0.5)
    ab_conv_w = nrm(ks[5], (N_EVEN, CONV_WIDTH, RG_WIDTH), CONV_WIDTH ** -0.5)
    ab_conv_b = nrm(ks[6], (N_EVEN, RG_WIDTH), 0.01)
    rg_wa = nrm(ks[7], (N_EVEN, RG_BLOCKS, RG_BLOCK_DIM, RG_BLOCK_DIM), RG_BLOCK_DIM ** -0.5)
    rg_ba = nrm(ks[8], (N_EVEN, RG_WIDTH), 0.01)
    rg_wx = nrm(ks[9], (N_EVEN, RG_BLOCKS, RG_BLOCK_DIM, RG_BLOCK_DIM), RG_BLOCK_DIM ** -0.5)
    rg_bx = nrm(ks[10], (N_EVEN, RG_WIDTH), 0.01)
    a_c = jax.random.uniform(ks[11], (N_EVEN, RG_WIDTH), F32, 0.9, 0.999)
    sig = a_c ** (1.0 / RG_C)
    rg_lambda = jnp.log(sig) - jnp.log1p(-sig)
    hg_lb_logits = nrm(ks[12], (N_EVEN + 1, HG_QK_WIDTH), 0.1).at[0].add(-2.0)
    hg_norm = 1.0 + nrm(ks[13], (N_EVEN, HG_V_WIDTH), 0.01)
    ab_w_out = nrm(ks[14], (N_EVEN, AB_OUT_WIDTH, D_MODEL), AB_OUT_WIDTH ** -0.5)
    c_w_in = nrm(ks[15], (N_ODD, D_MODEL, C_IN_WIDTH), D_MODEL ** -0.5)
    c_conv_w = nrm(ks[16], (N_ODD, CONV_WIDTH, GDN_QKV_WIDTH), CONV_WIDTH ** -0.5)
    c_a_log = jnp.log(jax.random.uniform(ks[17], (N_ODD, GDN_V_HEADS), F32, 1.0, 16.0))
    dt = jnp.exp(jax.random.uniform(ks[18], (N_ODD, GDN_V_HEADS), F32,
                                    float(np.log(1e-3)), float(np.log(1e-1))))
    c_dt_bias = dt + jnp.log(-jnp.expm1(-dt))
    c_norm = 1.0 + nrm(ks[19], (N_ODD, GDN_DV), 0.01)
    c_w_out = nrm(ks[20], (N_ODD, GDN_V_WIDTH, D_MODEL), GDN_V_WIDTH ** -0.5)
    moe_wg = nrm(ks[21], (DEPTH, D_MODEL, N_GROUPS), D_MODEL ** -0.5)
    moe_bg = nrm(ks[22], (DEPTH, N_GROUPS), ROUTER_BIAS_SCALE)
    moe_we = nrm(ks[23], (DEPTH, D_MODEL, N_EXPERTS), D_MODEL ** -0.5)
    moe_be = nrm(ks[24], (DEPTH, N_EXPERTS), ROUTER_BIAS_SCALE)
    moe_w_gate = nrm(ks[25], (DEPTH, N_EXPERTS, D_MODEL, D_EXPERT), D_MODEL ** -0.5)
    moe_w_up = nrm(ks[26], (DEPTH, N_EXPERTS, D_MODEL, D_EXPERT), D_MODEL ** -0.5)
    moe_w_down = nrm(ks[27], (DEPTH, N_EXPERTS, D_EXPERT, D_MODEL), D_EXPERT ** -0.5)
    return {'x': x, 'norm_mix': norm_mix, 'norm_ffn': norm_ffn, 'norm_final': norm_final,
            'ab_w_in': ab_w_in, 'ab_conv_w': ab_conv_w, 'ab_conv_b': ab_conv_b,
            'rg_wa': rg_wa, 'rg_ba': rg_ba, 'rg_wx': rg_wx, 'rg_bx': rg_bx, 'rg_lambda': rg_lambda,
            'hg_lb_logits': hg_lb_logits, 'hg_norm': hg_norm, 'ab_w_out': ab_w_out,
            'c_w_in': c_w_in, 'c_conv_w': c_conv_w, 'c_a_log': c_a_log, 'c_dt_bias': c_dt_bias,
            'c_norm': c_norm, 'c_w_out': c_w_out,
            'moe_wg': moe_wg, 'moe_bg': moe_bg, 'moe_we': moe_we, 'moe_be': moe_be,
            'moe_w_gate': moe_w_gate, 'moe_w_up': moe_w_up, 'moe_w_down': moe_w_down}


def reference(x, norm_mix, norm_ffn, norm_final, ab_w_in, ab_conv_w, ab_conv_b, rg_wa, rg_ba, rg_wx,
              rg_bx, rg_lambda, hg_lb_logits, hg_norm, ab_w_out, c_w_in, c_conv_w, c_a_log, c_dt_bias,
              c_norm, c_w_out, moe_wg, moe_bg, moe_we, moe_be, moe_w_gate, moe_w_up, moe_w_down):
    lb_all = jnp.cumsum(jax.nn.softmax(hg_lb_logits.astype(F32), axis=0), axis=0)
    for layer in range(DEPTH):
        j = layer // 2
        h = rms_norm(x, norm_mix[layer])
        if layer % 2 == 0:
            mix = mixer_rglru_hgrn2(h, ab_w_in[j], ab_conv_w[j], ab_conv_b[j], rg_wa[j], rg_ba[j],
                                    rg_wx[j], rg_bx[j], rg_lambda[j], lb_all[j], hg_norm[j], ab_w_out[j])
        else:
            mix = mixer_gated_deltanet(h, c_w_in[j], c_conv_w[j], c_a_log[j], c_dt_bias[j],
                                       c_norm[j], c_w_out[j])
        x = x + mix.astype(x.dtype)
        h = rms_norm(x, norm_ffn[layer])
        x = x + hierarchical_moe(h, moe_wg[layer], moe_bg[layer], moe_we[layer], moe_be[layer],
                                 moe_w_gate[layer], moe_w_up[layer], moe_w_down[layer]).astype(x.dtype)
    return rms_norm(x, norm_final)
```

```python
import functools

import jax
import jax.numpy as jnp
from jax import lax
from jax.experimental import pallas as pl
from jax.experimental.pallas import tpu as pltpu

F32 = jnp.float32
BF16 = jnp.bfloat16
I32 = jnp.int32
HIGHEST = lax.Precision.HIGHEST

NORM_EPS = 1e-6
CHUNK = 64
CHUNK_LOG2 = 6
SUB = 16
HEAD = 128
CONV_WIDTH = 4
CONV_HALO = 8
RG_C = 8.0
N_GROUPS = 8
EXPERTS_PER_GROUP = 8
N_EXPERTS = N_GROUPS * EXPERTS_PER_GROUP
TOP_K = 2
ROUTER_LANES = 128
MOE_BLOCK = 256
VMEM_LIMIT = 56 * 1024 * 1024


def _cparams(sem, vmem=VMEM_LIMIT):
    return pltpu.CompilerParams(dimension_semantics=sem, vmem_limit_bytes=vmem)


def _rmsnorm(x, gain):
    return x * lax.rsqrt(jnp.mean(x * x, axis=-1, keepdims=True) + NORM_EPS) * gain


def _sigmoid(x):
    return 1.0 / (1.0 + jnp.exp(-x))


def _silu(x):
    return x * _sigmoid(x)


def _softplus(x):
    return jnp.maximum(x, 0.0) + jnp.log(1.0 + jnp.exp(-jnp.abs(x)))


def _gelu_tanh(x):
    return 0.5 * x * (1.0 + jnp.tanh(0.7978845608028654 * (x + 0.044715 * (x * x * x))))


def _dot(a, b):
    return jnp.dot(a.astype(BF16), b.astype(BF16), preferred_element_type=F32)


def _dot_nt(a, b):
    return lax.dot_general(a.astype(BF16), b.astype(BF16), (((1,), (1,)), ((), ())),
                           preferred_element_type=F32)


def _dot_tn(a, b):
    return lax.dot_general(a.astype(BF16), b.astype(BF16), (((0,), (0,)), ((), ())),
                           preferred_element_type=F32)


def _dot_f32(a, b):
    return jnp.dot(a, b, precision=HIGHEST, preferred_element_type=F32)


def _norm_proj_body(*refs, has_small):
    if has_small:
        x_ref, g_ref, w_ref, ws_ref, o_ref, os_ref, h_ref = refs
    else:
        x_ref, g_ref, w_ref, o_ref, h_ref = refs

    @pl.when(pl.program_id(1) == 0)
    def _():
        hb = _rmsnorm(x_ref[...], g_ref[...]).astype(BF16)
        h_ref[...] = hb
        if has_small:
            os_ref[...] = jnp.dot(hb, ws_ref[...], preferred_element_type=F32)

    o_ref[...] = jnp.dot(h_ref[...], w_ref[...], preferred_element_type=F32).astype(o_ref.dtype)


def _norm_proj(x, gain, w, w_small=None, *, tm=1024, tn=512, out_dtype=F32):
    T, D = x.shape
    N = w.shape[1]
    tm = min(tm, T)
    has_small = w_small is not None
    in_specs = [pl.BlockSpec((tm, D), lambda i, j: (i, 0)),
                pl.BlockSpec((1, D), lambda i, j: (0, 0)),
                pl.BlockSpec((D, tn), lambda i, j: (0, j))]
    out_shape = [jax.ShapeDtypeStruct((T, N), out_dtype)]
    out_specs = [pl.BlockSpec((tm, tn), lambda i, j: (i, j))]
    args = [x, gain.reshape(1, D), w]
    if has_small:
        ns = w_small.shape[1]
        in_specs.append(pl.BlockSpec((D, ns), lambda i, j: (0, 0)))
        out_shape.append(jax.ShapeDtypeStruct((T, ns), F32))
        out_specs.append(pl.BlockSpec((tm, ns), lambda i, j: (i, 0)))
        args.append(w_small)
    res = pl.pallas_call(
        functools.partial(_norm_proj_body, has_small=has_small),
        out_shape=out_shape, grid=(T // tm, N // tn), in_specs=in_specs, out_specs=out_specs,
        scratch_shapes=[pltpu.VMEM((tm, D), BF16)],
        compiler_params=_cparams(("parallel", "arbitrary")), name="norm_proj",
    )(*args)
    return res if has_small else res[0]


def _proj_res_body(*refs, n_parts):
    ys = refs[:n_parts]
    ws = refs[n_parts:2 * n_parts]
    x_ref = refs[2 * n_parts]
    o_ref = refs[2 * n_parts + 1]
    acc = x_ref[...]
    for y_ref, w_ref in zip(ys, ws):
        acc = acc + jnp.dot(y_ref[...], w_ref[...], preferred_element_type=F32)
    o_ref[...] = acc


def _proj_residual(ys, ws, x, *, tm=512, tn=512):
    T, N = x.shape
    tm = min(tm, T)
    n = len(ys)
    in_specs = ([pl.BlockSpec((tm, y.shape[1]), lambda i, j: (i, 0)) for y in ys]
                + [pl.BlockSpec((w.shape[0], tn), lambda i, j: (0, j)) for w in ws]
                + [pl.BlockSpec((tm, tn), lambda i, j: (i, j))])
    return pl.pallas_call(
        functools.partial(_proj_res_body, n_parts=n),
        out_shape=jax.ShapeDtypeStruct((T, N), F32), grid=(T // tm, N // tn),
        in_specs=in_specs, out_specs=pl.BlockSpec((tm, tn), lambda i, j: (i, j)),
        compiler_params=_cparams(("parallel", "parallel")), name="proj_residual",
    )(*ys, *ws, x)


def _rg_body(x_ref, gate_ref, cw_ref, cb_ref, wa_ref, ba_ref, wx_ref, bx_ref, lam_ref, o_ref,
             xp_ref, h_ref, a_ref, u_ref):
    ts, W = x_ref.shape

    @pl.when(pl.program_id(1) == 0)
    def _():
        xp_ref[0:CONV_HALO, :] = jnp.zeros((CONV_HALO, W), F32)
        h_ref[...] = jnp.zeros(h_ref.shape, F32)

    xp_ref[CONV_HALO:, :] = x_ref[...]
    cw = cw_ref[...]
    xc = cb_ref[...] + xp_ref[pl.ds(CONV_HALO - 3, ts), :] * cw[0:1, :]
    for j in range(1, CONV_WIDTH):
        xc = xc + xp_ref[pl.ds(CONV_HALO - 3 + j, ts), :] * cw[j:j + 1, :]
    xp_ref[0:CONV_HALO, :] = xp_ref[ts:ts + CONV_HALO, :]

    neg_c_sp = -RG_C * _softplus(-lam_ref[...])
    for blk in range(W // HEAD):
        cs = slice(blk * HEAD, (blk + 1) * HEAD)
        xb = xc[:, cs]
        r = _sigmoid(_dot(xb, wa_ref[blk]) + ba_ref[:, cs])
        i = _sigmoid(_dot(xb, wx_ref[blk]) + bx_ref[:, cs])
        log_a = neg_c_sp[:, cs] * r
        a_ref[:, cs] = jnp.exp(log_a)
        u_ref[:, cs] = jnp.sqrt(1.0 - jnp.exp(2.0 * log_a)) * (i * xb)

    row = lax.broadcasted_iota(I32, (8, W), 0)

    def group(g, h):
        r0 = pl.multiple_of(g * 8, 8)
        A = a_ref[pl.ds(r0, 8), :]
        U = u_ref[pl.ds(r0, 8), :]
        for d in (1, 2, 4):
            keep = row >= d
            U = jnp.where(keep, A * pltpu.roll(U, d, 0) + U, U)
            A = jnp.where(keep, A * pltpu.roll(A, d, 0), A)
        hh = A * h + U
        u_ref[pl.ds(r0, 8), :] = hh
        return jnp.broadcast_to(hh[7:8, :], (8, W))

    h_ref[...] = lax.fori_loop(0, ts // 8, group, h_ref[...])
    o_ref[...] = (u_ref[...] * _gelu_tanh(gate_ref[...])).astype(o_ref.dtype)


def _rg_branch(proj, B, S, conv_w, conv_b, wa, ba, wx, bx, lam, *, ts=512):
    T = proj.shape[0]
    W = conv_w.shape[1]
    ts = min(ts, S)
    nS = S // ts
    row = lambda v: v.reshape(1, W)
    tok = lambda b, s: b * nS + s
    full2 = lambda shape: pl.BlockSpec(shape, lambda b, s: (0, 0))
    full3 = lambda shape: pl.BlockSpec(shape, lambda b, s: (0, 0, 0))
    return pl.pallas_call(
        _rg_body,
        out_shape=jax.ShapeDtypeStruct((T, W), BF16), grid=(B, nS),
        in_specs=[pl.BlockSpec((ts, W), lambda b, s: (tok(b, s), 0)),
                  pl.BlockSpec((ts, W), lambda b, s: (tok(b, s), 1)),
                  full2((CONV_WIDTH, W)), full2((1, W)),
                  full3(wa.shape), full2((1, W)), full3(wx.shape), full2((1, W)), full2((1, W))],
        out_specs=pl.BlockSpec((ts, W), lambda b, s: (tok(b, s), 0)),
        scratch_shapes=[pltpu.VMEM((ts + CONV_HALO, W), F32), pltpu.VMEM((8, W), F32),
                        pltpu.VMEM((ts, W), F32), pltpu.VMEM((ts, W), F32)],
        compiler_params=_cparams(("parallel", "arbitrary")), name="rg_lru",
    )(proj, proj, conv_w, row(conv_b), wa.astype(BF16), row(ba), wx.astype(BF16), row(bx), row(lam))


def _hg_body(q_ref, f_ref, v_ref, g_ref, lb_ref, gain_ref, o_ref, st_ref):
    ts = q_ref.shape[0]

    @pl.when(pl.program_id(2) == 0)
    def _():
        st_ref[...] = jnp.zeros(st_ref.shape, F32)

    lb = lb_ref[0]
    gain = gain_ref[0]
    r = lax.broadcasted_iota(I32, (CHUNK, CHUNK), 0)
    c = lax.broadcasted_iota(I32, (CHUNK, CHUNK), 1)
    tri = (r >= c).astype(F32)
    sub_row = lax.broadcasted_iota(I32, (SUB, 1), 0)
    ones = jnp.ones((HEAD, HEAD), BF16)
    n_sub = CHUNK // SUB

    def chunk(ci, carry):
        r0 = pl.multiple_of(ci * CHUNK, CHUNK)
        rows = pl.ds(r0, CHUNK)
        q = _silu(q_ref[rows, :])
        fg = lb + (1.0 - lb) * _sigmoid(f_ref[rows, :])
        k = 1.0 - fg
        b = _dot_f32(tri, jnp.log(fg))
        v = v_ref[rows, :]
        st_t = st_ref[...]
        inter = _dot_nt(q * jnp.exp(b), st_t)
        outs = []
        for bi in range(n_sub):
            lo = bi * SUB
            sl = slice(lo, lo + SUB)
            q_i, k_i, b_i, v_i = q[sl], k[sl], b[sl], v[sl]
            acc = inter[sl]
            if bi > 0:
                b_ref = b[lo:lo + 1]
                q_t = q_i * jnp.exp(b_i - b_ref)
                k_t = k[:lo] * jnp.exp(b_ref - b[:lo])
                acc = acc + _dot(_dot_nt(q_t, k_t), v[:lo])
            parts = []
            for s in range(SUB):
                e = jnp.exp(jnp.where(sub_row >= s, b_i - b_i[s:s + 1], -jnp.inf))
                parts.append((q_i * k_i[s:s + 1]) * e)
            srep = _dot(jnp.concatenate(parts, axis=0), ones)
            for s in range(SUB):
                acc = acc + srep[s * SUB:(s + 1) * SUB] * v_i[s:s + 1]
            outs.append(acc)
        o = jnp.concatenate(outs, axis=0)
        b_last = b[CHUNK - 1:CHUNK]
        st_ref[...] = st_t * jnp.exp(b_last) + _dot_tn(v, k * jnp.exp(b_last - b))
        y = _rmsnorm(o, gain) * _silu(g_ref[rows, :])
        o_ref[rows, :] = y.astype(o_ref.dtype)
        return carry

    lax.fori_loop(0, ts // CHUNK, chunk, 0)


def _hgrn2_branch(proj, B, S, col_q, col_f, col_v, col_g, lb, gain, *, ts=512):
    T = proj.shape[0]
    H = lb.shape[0] // HEAD
    ts = min(ts, S)
    nS = S // ts
    spec = lambda col: pl.BlockSpec((ts, HEAD), lambda b, h, s: (b * nS + s, col + h))
    head_row = pl.BlockSpec((1, 1, HEAD), lambda b, h, s: (h, 0, 0))
    return pl.pallas_call(
        _hg_body,
        out_shape=jax.ShapeDtypeStruct((T, H * HEAD), BF16), grid=(B, H, nS),
        in_specs=[spec(col_q), spec(col_f), spec(col_v), spec(col_g), head_row, head_row],
        out_specs=pl.BlockSpec((ts, HEAD), lambda b, h, s: (b * nS + s, h)),
        scratch_shapes=[pltpu.VMEM((HEAD, HEAD), F32)],
        compiler_params=_cparams(("parallel", "parallel", "arbitrary")), name="hgrn2",
    )(proj, proj, proj, proj, lb.reshape(H, 1, HEAD), gain.reshape(H, 1, HEAD))


def _gdn_pre_body(x_ref, halo_ref, cw_ref, o_ref, xp_ref, *, n_q_blocks, n_qk_blocks, q_scale):
    ts, cb = x_ref.shape
    s = pl.program_id(1)
    j = pl.program_id(2)
    xp_ref[0:CONV_HALO, :] = jnp.where(s == 0, 0.0, halo_ref[...])
    xp_ref[CONV_HALO:, :] = x_ref[...]
    cw = cw_ref[...]
    y = xp_ref[pl.ds(CONV_HALO - 3, ts), :] * cw[0:1, :]
    for t in range(1, CONV_WIDTH):
        y = y + xp_ref[pl.ds(CONV_HALO - 3 + t, ts), :] * cw[t:t + 1, :]
    y = _silu(y)
    scale = jnp.where(j < n_q_blocks, q_scale, 1.0)
    for hd in range(cb // HEAD):
        cs = slice(hd * HEAD, (hd + 1) * HEAD)
        yh = y[:, cs]
        inv = lax.rsqrt(jnp.sum(yh * yh, axis=-1, keepdims=True) + 1e-6) * scale
        o_ref[:, cs] = (yh * jnp.where(j < n_qk_blocks, inv, 1.0)).astype(o_ref.dtype)


def _gdn_pre(proj, B, S, conv_w, qk_width, *, ts=512, cb=512):
    T = proj.shape[0]
    Wc = conv_w.shape[1]
    ts = min(ts, S)
    nS = S // ts
    hb = ts // CONV_HALO
    body = functools.partial(_gdn_pre_body, n_q_blocks=qk_width // cb, n_qk_blocks=2 * qk_width // cb,
                             q_scale=float(HEAD) ** -0.5)
    return pl.pallas_call(
        body,
        out_shape=jax.ShapeDtypeStruct((T, Wc), BF16), grid=(B, nS, Wc // cb),
        in_specs=[pl.BlockSpec((ts, cb), lambda b, s, j: (b * nS + s, j)),
                  pl.BlockSpec((CONV_HALO, cb), lambda b, s, j: (jnp.maximum((b * nS + s) * hb - 1, 0), j)),
                  pl.BlockSpec((CONV_WIDTH, cb), lambda b, s, j: (0, j))],
        out_specs=pl.BlockSpec((ts, cb), lambda b, s, j: (b * nS + s, j)),
        scratch_shapes=[pltpu.VMEM((ts + CONV_HALO, cb), F32)],
        compiler_params=_cparams(("parallel", "parallel", "parallel")), name="gdn_conv",
    )(proj, proj, conv_w)


def _unit_lower_inverse(low):
    r = lax.broadcasted_iota(I32, (CHUNK, CHUNK), 0)
    c = lax.broadcasted_iota(I32, (CHUNK, CHUNK), 1)
    eye = (r == c).astype(F32)
    same16 = (r >> 4) == (c >> 4)
    same32 = (r >> 5) == (c >> 5)
    p = jnp.where(same16, -low, 0.0)
    x = eye + p
    for _ in range(3):
        p = _dot(p, p)
        x = x + _dot(x, p)
    for mask in (same32 & ~same16, ~same32):
        x = x - _dot(_dot(x, jnp.where(mask, low, 0.0)), x)
    return x


def _gdn_body(alog_ref, dtb_ref, q_ref, k_ref, v_ref, z_ref, sm_ref, brow_ref, arow_ref, gain_ref,
              o_ref, st_ref, gc_ref, beta_ref, gcrow_ref, betarow_ref, *, n_heads):
    ts = q_ref.shape[0]
    hv = pl.program_id(1)

    @pl.when(pl.program_id(2) == 0)
    def _():
        st_ref[...] = jnp.zeros(st_ref.shape, F32)

    neg_a = -jnp.exp(jnp.full((1, 1), alog_ref[hv], F32))
    dtb = jnp.full((1, 1), dtb_ref[hv], F32)

    sm = sm_ref[...]
    lane = lax.broadcasted_iota(I32, sm.shape, 1)
    pick = lambda idx: jnp.sum(jnp.where(lane == idx, sm, 0.0), axis=1, keepdims=True)
    beta_ref[...] = jnp.broadcast_to(_sigmoid(pick(hv)), (ts, HEAD))
    g_col = jnp.broadcast_to(neg_a * _softplus(pick(n_heads + hv) + dtb), (ts, HEAD))
    r = lax.broadcasted_iota(I32, (ts, ts), 0)
    c = lax.broadcasted_iota(I32, (ts, ts), 1)
    blk_tri = (((r >> CHUNK_LOG2) == (c >> CHUNK_LOG2)) & (r >= c)).astype(F32)
    gc_ref[...] = _dot_f32(blk_tri, g_col)
    betarow_ref[...] = _sigmoid(brow_ref[...])
    g_row = neg_a * _softplus(arow_ref[...] + dtb)
    rr = lax.broadcasted_iota(I32, (CHUNK, CHUNK), 0)
    cc = lax.broadcasted_iota(I32, (CHUNK, CHUNK), 1)
    gcrow_ref[...] = _dot_f32(g_row, (rr <= cc).astype(F32))
    gain = gain_ref[...]

    def chunk(ci, carry):
        r0 = pl.multiple_of(ci * CHUNK, CHUNK)
        rows = pl.ds(r0, CHUNK)
        q = q_ref[rows, :].astype(F32)
        k = k_ref[rows, :].astype(F32)
        v = v_ref[rows, :].astype(F32)
        beta = beta_ref[rows, :]
        gc = gc_ref[rows, :]
        gc_r = gcrow_ref[pl.ds(ci, 1), :]
        beta_c = beta[:, :CHUNK]
        decay = jnp.exp(jnp.where(rr >= cc, gc[:, :CHUNK] - gc_r, -jnp.inf))
        kk = _dot_nt(k, k)
        t_inv = _unit_lower_inverse(jnp.where(rr > cc, beta_c * kk * decay, 0.0))
        e_gc = jnp.exp(gc)
        u = _dot(t_inv, v * beta)
        w = _dot(t_inv, k * (beta * e_gc))
        qk = _dot_nt(q, k) * decay
        state = st_ref[...]
        delta = u - _dot(w, state)
        o = _dot(q * e_gc, state) + _dot(qk, delta)
        gc_last = gc[CHUNK - 1:CHUNK]
        st_ref[...] = jnp.exp(gc_last) * state + _dot_tn(k * jnp.exp(gc_last - gc), delta)
        y = _rmsnorm(o, gain) * _silu(z_ref[rows, :])
        o_ref[rows, :] = y.astype(o_ref.dtype)
        return carry

    lax.fori_loop(0, ts // CHUNK, chunk, 0)


def _gdn_branch(qkv, proj, small, B, S, col_z, a_log, dt_bias, gain, *, ts=512):
    T = qkv.shape[0]
    n_heads = a_log.shape[0]
    n_k_heads = n_heads // 2
    ts = min(ts, S)
    nS = S // ts
    nc = ts // CHUNK
    rows = small.T.reshape(small.shape[1], T // CHUNK, CHUNK)
    tokspec = lambda colfn: pl.BlockSpec((ts, HEAD), lambda b, h, s, *_: (b * nS + s, colfn(h)))
    rowspec = lambda off: pl.BlockSpec((None, nc, CHUNK), lambda b, h, s, *_: (off + h, b * nS + s, 0))
    grid_spec = pltpu.PrefetchScalarGridSpec(
        num_scalar_prefetch=2, grid=(B, n_heads, nS),
        in_specs=[tokspec(lambda h: h // 2), tokspec(lambda h: n_k_heads + h // 2),
                  tokspec(lambda h: 2 * n_k_heads + h), tokspec(lambda h: col_z + h),
                  pl.BlockSpec((ts, small.shape[1]), lambda b, h, s, *_: (b * nS + s, 0)),
                  rowspec(0), rowspec(n_heads),
                  pl.BlockSpec((1, HEAD), lambda b, h, s, *_: (0, 0))],
        out_specs=tokspec(lambda h: h),
        scratch_shapes=[pltpu.VMEM((HEAD, HEAD), F32), pltpu.VMEM((ts, HEAD), F32),
                        pltpu.VMEM((ts, HEAD), F32), pltpu.VMEM((nc, CHUNK), F32),
                        pltpu.VMEM((nc, CHUNK), F32)])
    return pl.pallas_call(
        functools.partial(_gdn_body, n_heads=n_heads),
        out_shape=jax.ShapeDtypeStruct((T, n_heads * HEAD), BF16), grid_spec=grid_spec,
        compiler_params=_cparams(("parallel", "parallel", "arbitrary")), name="gdn",
    )(a_log, dt_bias, qkv, qkv, qkv, proj, small, rows, rows, gain.reshape(1, HEAD))


def _router_body(x_ref, g_ref, w_ref, b_ref, h_ref, mf_ref, mi_ref, cnt_ref, run_ref):
    tm = x_ref.shape[0]

    @pl.when(pl.program_id(0) == 0)
    def _():
        run_ref[...] = jnp.zeros(run_ref.shape, F32)

    h = _rmsnorm(x_ref[...], g_ref[...])
    h_ref[...] = h
    z = _dot_f32(h, w_ref[...]) + b_ref[...]
    lane_i = lax.broadcasted_iota(I32, z.shape, 1)
    lane = lane_i.astype(F32)
    big = float(ROUTER_LANES)
    rmax = lambda a: jnp.max(a, axis=1, keepdims=True)
    first = lambda hit: jnp.min(jnp.where(hit, lane, big), axis=1, keepdims=True)

    is_grp = lane_i < N_GROUPS
    zg = jnp.where(is_grp, z, -jnp.inf)
    mg = rmax(zg)
    grp_p = 1.0 / jnp.sum(jnp.where(is_grp, jnp.exp(zg - mg), 0.0), axis=1, keepdims=True)
    gidx = first(zg == mg)

    exp_lane = lane - float(N_GROUPS)
    in_grp = ((lane_i >= N_GROUPS) & (lane_i < N_GROUPS + N_EXPERTS)
              & (jnp.floor(exp_lane * (1.0 / EXPERTS_PER_GROUP)) == gidx))
    ze = jnp.where(in_grp, z, -jnp.inf)
    t1 = rmax(ze)
    i1 = first(ze == t1)
    ze2 = jnp.where(lane == i1, -jnp.inf, ze)
    t2 = rmax(ze2)
    i2 = first(ze2 == t2)
    e21 = jnp.exp(t2 - t1)
    g1 = grp_p / (1.0 + e21)
    g2 = g1 * e21

    onehot = ((lane == i1) | (lane == i2)).astype(F32)
    r = lax.broadcasted_iota(I32, (tm, tm), 0)
    c = lax.broadcasted_iota(I32, (tm, tm), 1)
    before = _dot((r > c).astype(BF16), onehot) + run_ref[0:1, :]
    rank = lambda idx: jnp.sum(jnp.where(lane == idx, before, 0.0), axis=1, keepdims=True)
    rk1, rk2 = rank(i1), rank(i2)
    run_ref[...] = run_ref[...] + jnp.sum(onehot, axis=0, keepdims=True)
    cnt_ref[...] = run_ref[...]

    mf_ref[...] = jnp.where(lane_i == 0, g1, jnp.where(lane_i == 1, g2, 0.0))
    meta = jnp.where(lane_i == 0, i1 - N_GROUPS,
                     jnp.where(lane_i == 1, i2 - N_GROUPS,
                               jnp.where(lane_i == 2, rk1, jnp.where(lane_i == 3, rk2, 0.0))))
    mi_ref[...] = meta.astype(I32)


def _router(x, gain, w_router, b_router, *, tm=512):
    T, D = x.shape
    tm = min(tm, T)
    tile = lambda n: pl.BlockSpec((tm, n), lambda i: (i, 0))
    full = lambda shape: pl.BlockSpec(shape, lambda i: (0, 0))
    return pl.pallas_call(
        _router_body,
        out_shape=[jax.ShapeDtypeStruct((T, D), F32), jax.ShapeDtypeStruct((T, ROUTER_LANES), F32),
                   jax.ShapeDtypeStruct((T, ROUTER_LANES), I32), jax.ShapeDtypeStruct((8, ROUTER_LANES), F32)],
        grid=(T // tm,),
        in_specs=[tile(D), full((1, D)), full((D, ROUTER_LANES)), full((1, ROUTER_LANES))],
        out_specs=[tile(D), tile(ROUTER_LANES), tile(ROUTER_LANES), full((8, ROUTER_LANES))],
        scratch_shapes=[pltpu.VMEM((8, ROUTER_LANES), F32)],
        compiler_params=_cparams(("arbitrary",)), name="moe_router",
    )(x, gain.reshape(1, D), w_router, b_router)


def _dispatch_body(dest_ref, h_ref, xs_in_ref, xs_ref, sem, *, td):
    del xs_in_ref
    base = pl.program_id(0) * td

    def copy(tok, dst):
        return pltpu.make_async_copy(h_ref.at[pl.ds(tok, 1)], xs_ref.at[pl.ds(dst, 1)], sem)

    def issue(t, carry):
        tok = base + t
        for kk in range(TOP_K):
            copy(tok, dest_ref[TOP_K * tok + kk]).start()
        return carry

    def drain(t, carry):
        for _ in range(TOP_K):
            copy(0, 0).wait()
        return carry

    lax.fori_loop(0, td, issue, 0)
    lax.fori_loop(0, td, drain, 0)


def _dispatch(dest, h, n_rows, *, td=256):
    T, D = h.shape
    td = min(td, T)
    grid_spec = pltpu.PrefetchScalarGridSpec(
        num_scalar_prefetch=1, grid=(T // td,),
        in_specs=[pl.BlockSpec(memory_space=pl.ANY), pl.BlockSpec(memory_space=pl.ANY)],
        out_specs=pl.BlockSpec(memory_space=pl.ANY),
        scratch_shapes=[pltpu.SemaphoreType.DMA(())])
    return pl.pallas_call(
        functools.partial(_dispatch_body, td=td),
        out_shape=jax.ShapeDtypeStruct((n_rows, D), F32), grid_spec=grid_spec,
        input_output_aliases={2: 0},
        compiler_params=_cparams(("arbitrary",)), name="moe_dispatch",
    )(dest, h, jnp.zeros((n_rows, D), F32))


def _ffn_body(be_ref, nu_ref, x_ref, wg_ref, wu_ref, wd_ref, o_ref):
    del be_ref

    @pl.when(pl.program_id(0) < nu_ref[0])
    def _():
        x = x_ref[...].astype(BF16)
        gate = jnp.dot(x, wg_ref[...].astype(BF16), preferred_element_type=F32)
        up = jnp.dot(x, wu_ref[...].astype(BF16), preferred_element_type=F32)
        hid = (_silu(gate) * up).astype(BF16)
        o_ref[...] = jnp.dot(hid, wd_ref[...].astype(BF16), preferred_element_type=F32)

    @pl.when(pl.program_id(0) >= nu_ref[0])
    def _():
        o_ref[...] = jnp.zeros(o_ref.shape, F32)


def _expert_ffn(block_expert, n_used, xs, w_gate, w_up, w_down):
    R, D = xs.shape
    E, _, F = w_gate.shape
    n_blocks = R // MOE_BLOCK
    rows = lambda i, be, nu: (jnp.minimum(i, nu[0] - 1), 0)
    wspec = lambda shape: pl.BlockSpec((None,) + shape, lambda i, be, nu: (be[i], 0, 0))
    grid_spec = pltpu.PrefetchScalarGridSpec(
        num_scalar_prefetch=2, grid=(n_blocks,),
        in_specs=[pl.BlockSpec((MOE_BLOCK, D), rows), wspec((D, F)), wspec((D, F)), wspec((F, D))],
        out_specs=pl.BlockSpec((MOE_BLOCK, D), lambda i, be, nu: (i, 0)))
    return pl.pallas_call(
        _ffn_body, out_shape=jax.ShapeDtypeStruct((R, D), F32), grid_spec=grid_spec,
        compiler_params=_cparams(("arbitrary",)), name="moe_ffn",
    )(block_expert, n_used, xs, w_gate, w_up, w_down)


def _combine_body(dest_ref, x_ref, mf_ref, g_ref, ys_ref, o_ref, y_ref, sem, *, tc, final_norm):
    base = pl.program_id(0) * tc

    def copy(src, kk, t):
        return pltpu.make_async_copy(ys_ref.at[pl.ds(src, 1)], y_ref.at[kk, pl.ds(t, 1)], sem)

    def issue(t, carry):
        for kk in range(TOP_K):
            copy(dest_ref[TOP_K * (base + t) + kk], kk, t).start()
        return carry

    def drain(t, carry):
        for kk in range(TOP_K):
            copy(0, kk, 0).wait()
        return carry

    lax.fori_loop(0, tc, issue, 0)
    lax.fori_loop(0, tc, drain, 0)
    mf = mf_ref[...]
    out = x_ref[...] + mf[:, 0:1] * y_ref[0] + mf[:, 1:2] * y_ref[1]
    if final_norm:
        out = _rmsnorm(out, g_ref[...])
    o_ref[...] = out


def _combine(dest, x, meta_f, ys, final_gain, *, tc=256):
    T, D = x.shape
    tc = min(tc, T)
    final_norm = final_gain is not None
    gain = (final_gain if final_norm else jnp.ones((D,), F32)).reshape(1, D)
    grid_spec = pltpu.PrefetchScalarGridSpec(
        num_scalar_prefetch=1, grid=(T // tc,),
        in_specs=[pl.BlockSpec((tc, D), lambda i, d: (i, 0)),
                  pl.BlockSpec((tc, ROUTER_LANES), lambda i, d: (i, 0)),
                  pl.BlockSpec((1, D), lambda i, d: (0, 0)),
                  pl.BlockSpec(memory_space=pl.ANY)],
        out_specs=pl.BlockSpec((tc, D), lambda i, d: (i, 0)),
        scratch_shapes=[pltpu.VMEM((TOP_K, tc, D), F32), pltpu.SemaphoreType.DMA(())])
    return pl.pallas_call(
        functools.partial(_combine_body, tc=tc, final_norm=final_norm),
        out_shape=jax.ShapeDtypeStruct((T, D), F32), grid_spec=grid_spec,
        compiler_params=_cparams(("arbitrary",)), name="moe_combine",
    )(dest, x, meta_f, gain, ys)


def _moe_layer(x, norm_gain, w_grp, b_grp, w_exp, b_exp, w_gate, w_up, w_down, final_gain):
    T, D = x.shape
    pad = ROUTER_LANES - N_GROUPS - N_EXPERTS
    w_router = jnp.concatenate([w_grp, w_exp, jnp.zeros((D, pad), F32)], axis=1)
    b_router = jnp.concatenate([b_grp, b_exp, jnp.zeros((pad,), F32)]).reshape(1, ROUTER_LANES)
    h, meta_f, meta_i, counts = _router(x, norm_gain, w_router, b_router)

    cnt = counts[0, N_GROUPS:N_GROUPS + N_EXPERTS].astype(I32)
    padded = (cnt + MOE_BLOCK - 1) // MOE_BLOCK * MOE_BLOCK
    pad_end = jnp.cumsum(padded)
    pad_start = pad_end - padded
    dest = (pad_start[meta_i[:, 0:TOP_K]] + meta_i[:, TOP_K:2 * TOP_K]).reshape(T * TOP_K)
    n_blocks = (T * TOP_K) // MOE_BLOCK + N_EXPERTS
    n_used = (pad_end[-1] // MOE_BLOCK).astype(I32)
    blk = jnp.arange(n_blocks, dtype=I32)
    block_expert = jnp.minimum(jnp.searchsorted(pad_end, blk * MOE_BLOCK, side='right'), N_EXPERTS - 1)
    block_expert = jnp.where(blk < n_used, block_expert, block_expert[n_used - 1]).astype(I32)

    xs = _dispatch(dest, h, n_blocks * MOE_BLOCK)
    ys = _expert_ffn(block_expert, n_used.reshape(1), xs, w_gate, w_up, w_down)
    return _combine(dest, x, meta_f, ys, final_gain)


def _layer_rglru_hgrn2(x, B, S, gain, w_in, conv_w, conv_b, w_a, b_a, w_x, b_x, lam, lb, hg_gain, w_out):
    rg_w = conv_w.shape[1]
    hg_w = lb.shape[0]
    proj = _norm_proj(x, gain, w_in.astype(BF16))
    y_a = _rg_branch(proj, B, S, conv_w, conv_b, w_a, b_a, w_x, b_x, lam)
    c0 = 2 * rg_w // HEAD
    nh = hg_w // HEAD
    y_b = _hgrn2_branch(proj, B, S, c0, c0 + nh, c0 + 2 * nh, c0 + 3 * nh, lb, hg_gain)
    w_out = w_out.astype(BF16)
    return _proj_residual([y_a, y_b], [w_out[:rg_w], w_out[rg_w:]], x)


def _layer_gdn(x, B, S, gain, w_in, conv_w, a_log, dt_bias, out_gain, w_out):
    D = x.shape[1]
    n_heads = a_log.shape[0]
    qkv_w = conv_w.shape[1]
    v_w = n_heads * HEAD
    qk_w = (qkv_w - v_w) // 2
    main_w = qkv_w + v_w
    w_in_b = w_in.astype(BF16)
    w_small = jnp.concatenate([w_in_b[:, main_w:], jnp.zeros((D, HEAD - 2 * n_heads), BF16)], axis=1)
    proj, small = _norm_proj(x, gain, w_in_b[:, :main_w], w_small)
    qkv = _gdn_pre(proj, B, S, conv_w, qk_w)
    y = _gdn_branch(qkv, proj, small, B, S, qkv_w // HEAD, a_log, dt_bias, out_gain)
    return _proj_residual([y], [w_out.astype(BF16)], x)


def kernel(x, norm_mix, norm_ffn, norm_final, ab_w_in, ab_conv_w, ab_conv_b, rg_wa, rg_ba, rg_wx,
           rg_bx, rg_lambda, hg_lb_logits, hg_norm, ab_w_out, c_w_in, c_conv_w, c_a_log, c_dt_bias,
           c_norm, c_w_out, moe_wg, moe_bg, moe_we, moe_be, moe_w_gate, moe_w_up, moe_w_down):
    B, S, D = x.shape
    depth = norm_mix.shape[0]
    lb_all = jnp.cumsum(jax.nn.softmax(hg_lb_logits.astype(F32), axis=0), axis=0)
    xt = x.reshape(B * S, D)
    for layer in range(depth):
        j = layer // 2
        if layer % 2 == 0:
            xt = _layer_rglru_hgrn2(xt, B, S, norm_mix[layer], ab_w_in[j], ab_conv_w[j], ab_conv_b[j],
                                    rg_wa[j], rg_ba[j], rg_wx[j], rg_bx[j], rg_lambda[j], lb_all[j],
                                    hg_norm[j], ab_w_out[j])
        else:
            xt = _layer_gdn(xt, B, S, norm_mix[layer], c_w_in[j], c_conv_w[j], c_a_log[j], c_dt_bias[j],
                            c_norm[j], c_w_out[j])
        final_gain = norm_final if layer == depth - 1 else None
        xt = _moe_layer(xt, norm_ffn[layer], moe_wg[layer], moe_bg[layer], moe_we[layer], moe_be[layer],
                        moe_w_gate[layer], moe_w_up[layer], moe_w_down[layer], final_gain)
    return xt.reshape(B, S, D)
```
